```python
import jax, jax.numpy as jnp
from jax import lax
import numpy as np

D_MODEL = 1024
BATCH = 8
SEQ = 2048
DEPTH = 1

HEAD_DIM = 64
ATTN_HEADS = 8
ATTN_KV_HEADS = 2
ATTN_GROUP = ATTN_HEADS // ATTN_KV_HEADS
WINDOW = 128
ATTN_WIDTH = ATTN_HEADS * HEAD_DIM
ATTN_KV_WIDTH = ATTN_KV_HEADS * HEAD_DIM
RET_HEADS = 4
RET_HEAD_DIM = 128
RET_WIDTH = RET_HEADS * RET_HEAD_DIM
RET_CHUNK = 128
PEER_HEADS = 8
PEER_N_KEYS = 128
PEER_N_EXPERTS = PEER_N_KEYS * PEER_N_KEYS
PEER_QUERY_DIM = 256
PEER_HALF_DIM = PEER_QUERY_DIM // 2
PEER_TOPK = 16
PEER_BLOCK = 128
DEEPNORM_ALPHA = (2 * DEPTH) ** 0.25
DEEPNORM_BETA = (8 * DEPTH) ** -0.25
LN_EPS = 1e-5
GN_EPS = 1e-5
IN_SPLITS = (ATTN_WIDTH, ATTN_KV_WIDTH, ATTN_KV_WIDTH, RET_WIDTH, RET_WIDTH, RET_WIDTH, RET_WIDTH, D_MODEL, D_MODEL)
IN_WIDTH = sum(IN_SPLITS)
V_COLUMN_GROUPS = (2, 5)

kernel_name = "hybrid_swa_retention_peer_deepnorm"


def layer_norm(x, gain, bias):
    xf = x.astype(jnp.float32)
    mu = jnp.mean(xf, axis=-1, keepdims=True)
    var = jnp.mean(jnp.square(xf - mu), axis=-1, keepdims=True)
    y = (xf - mu) * lax.rsqrt(var + LN_EPS) * gain.astype(jnp.float32) + bias.astype(jnp.float32)
    return y.astype(x.dtype)


def alibi_slopes(n_heads):
    return 2.0 ** (-8.0 * jnp.arange(1, n_heads + 1, dtype=jnp.float32) / n_heads)


def sliding_window_attention(q, k, v, sinks):
    B, S = q.shape[:2]
    nb = S // WINDOW
    qb = q.reshape(B, nb, WINDOW, ATTN_KV_HEADS, ATTN_GROUP, HEAD_DIM)
    kb = k.reshape(B, nb, WINDOW, ATTN_KV_HEADS, HEAD_DIM)
    vb = v.reshape(B, nb, WINDOW, ATTN_KV_HEADS, HEAD_DIM)
    pad = ((0, 0), (1, 0), (0, 0), (0, 0), (0, 0))
    kk = jnp.concatenate([jnp.pad(kb, pad)[:, :-1], kb], axis=2)
    vv = jnp.concatenate([jnp.pad(vb, pad)[:, :-1], vb], axis=2)
    scores = jnp.einsum("bnqkgd,bnskd->bnkgqs", qb, kk).astype(jnp.float32)
    dist = (jnp.arange(WINDOW)[:, None] + WINDOW) - jnp.arange(2 * WINDOW)[None, :]
    in_band = (dist >= 0) & (dist < WINDOW)
    has_prev = (jnp.arange(nb)[:, None, None] > 0) | (jnp.arange(2 * WINDOW)[None, None, :] >= WINDOW)
    valid = in_band[None] & has_prev
    slopes = alibi_slopes(ATTN_HEADS).reshape(ATTN_KV_HEADS, ATTN_GROUP, 1, 1)
    scores = scores - slopes * dist.astype(jnp.float32)
    scores = jnp.where(valid[None, :, None, None], scores, -jnp.inf)
    sink = jnp.broadcast_to(sinks.astype(jnp.float32).reshape(ATTN_KV_HEADS, ATTN_GROUP, 1, 1),
                            scores.shape[:-1] + (1,))
    probs = jax.nn.softmax(jnp.concatenate([scores, sink], axis=-1), axis=-1)[..., :-1]
    out = jnp.einsum("bnkgqs,bnskd->bnqkgd", probs.astype(v.dtype), vv)
    return out.reshape(B, S, ATTN_WIDTH)


def retention_chunkwise(q, k, v):
    B, S, H, _ = q.shape
    Dv = v.shape[-1]
    nc = S // RET_CHUNK
    f32 = jnp.float32
    log_gamma = jnp.log1p(-(2.0 ** (-5.0 - jnp.arange(H, dtype=f32))))
    pos = jnp.arange(RET_CHUNK, dtype=f32)
    diff = pos[:, None] - pos[None, :]
    inner_decay = jnp.where(diff >= 0, jnp.exp(jnp.maximum(diff, 0.0) * log_gamma[:, None, None]), 0.0)
    k_decay = jnp.exp((RET_CHUNK - 1.0 - pos) * log_gamma[:, None])
    q_decay = jnp.exp((pos + 1.0) * log_gamma[:, None])
    chunk_decay = jnp.exp(RET_CHUNK * log_gamma)

    def to_chunks(t):
        return t.astype(f32).reshape(B, nc, RET_CHUNK, H, t.shape[-1]).transpose(1, 0, 3, 2, 4)

    qc, kc, vc = to_chunks(q), to_chunks(k), to_chunks(v)
    inner = jnp.einsum("nbhij,nbhje->nbhie", jnp.einsum("nbhid,nbhjd->nbhij", qc, kc) * inner_decay, vc)
    chunk_kv = jnp.einsum("nbhjd,hj,nbhje->nbhde", kc, k_decay, vc)

    def step(state, kv):
        return chunk_decay[None, :, None, None] * state + kv, state

    _, prev = lax.scan(step, jnp.zeros(chunk_kv.shape[1:], f32), chunk_kv)
    cross = jnp.einsum("nbhid,nbhde->nbhie", qc, prev) * q_decay[:, :, None]
    return (inner + cross).transpose(1, 0, 3, 2, 4).reshape(B, S, H, Dv)


def head_group_norm(y, gain):
    B, S = y.shape[:2]
    mu = jnp.mean(y, axis=-1, keepdims=True)
    var = jnp.mean(jnp.square(y - mu), axis=-1, keepdims=True)
    return ((y - mu) * lax.rsqrt(var + GN_EPS)).reshape(B, S, RET_WIDTH) * gain.astype(jnp.float32)


def peer(x, w_query, sub_keys, expert_down, expert_up):
    B, S, D = x.shape
    T = B * S
    xt = x.reshape(T, D)
    q = (xt @ w_query).reshape(T, PEER_HEADS, 2, PEER_HALF_DIM)
    half_scores = jnp.einsum("thpd,pnd->thpn", q, sub_keys).astype(jnp.float32)
    half_vals, half_idx = lax.top_k(half_scores, PEER_TOPK)
    n_cand = PEER_TOPK * PEER_TOPK
    cand_scores = (half_vals[:, :, 0, :, None] + half_vals[:, :, 1, None, :]).reshape(T, PEER_HEADS, n_cand)
    cand_idx = (half_idx[:, :, 0, :, None] * PEER_N_KEYS + half_idx[:, :, 1, None, :]).reshape(T, PEER_HEADS, n_cand)
    top_scores, top_pos = lax.top_k(cand_scores, PEER_TOPK)
    experts = jnp.take_along_axis(cand_idx, top_pos, axis=-1)
    gates = jax.nn.softmax(top_scores, axis=-1).astype(x.dtype)
    nblk = T // PEER_BLOCK

    def block(args):
        xb, eb, gb = args
        down = jnp.take(expert_down, eb, axis=0)
        up = jnp.take(expert_up, eb, axis=0)
        act = jax.nn.gelu(jnp.einsum("phkd,pd->phk", down, xb), approximate=False)
        return jnp.einsum("phk,phkd->pd", gb * act, up)

    out = lax.map(block, (xt.reshape(nblk, PEER_BLOCK, D),
                          experts.reshape(nblk, PEER_BLOCK, PEER_HEADS, PEER_TOPK),
                          gates.reshape(nblk, PEER_BLOCK, PEER_HEADS, PEER_TOPK)))
    return out.reshape(B, S, D)


def setup_inputs(seed: int = 0) -> dict:
    key = jax.random.key(seed)
    ks = jax.random.split(key, 16)
    f32 = jnp.float32

    def normal(k, shape, scale):
        return jax.random.normal(k, shape, f32) * scale

    col_scale = np.concatenate([np.full((w,), DEEPNORM_BETA if i in V_COLUMN_GROUPS else 1.0, np.float32)
                                for i, w in enumerate(IN_SPLITS)])
    return {
        "x": normal(ks[0], (BATCH, SEQ, D_MODEL), 1.0),
        "w_in": normal(ks[1], (DEPTH, D_MODEL, IN_WIDTH), D_MODEL ** -0.5) * jnp.asarray(col_scale),
        "attn_sinks": normal(ks[2], (DEPTH, ATTN_HEADS), 0.5),
        "ret_gn_gain": 1.0 + normal(ks[3], (DEPTH, RET_WIDTH), 0.02),
        "w_attn_branch": normal(ks[4], (DEPTH, ATTN_WIDTH, D_MODEL), DEEPNORM_BETA * ATTN_WIDTH ** -0.5),
        "w_ret_branch": normal(ks[5], (DEPTH, RET_WIDTH, D_MODEL), DEEPNORM_BETA * RET_WIDTH ** -0.5),
        "w_out": normal(ks[6], (DEPTH, D_MODEL, D_MODEL), DEEPNORM_BETA * D_MODEL ** -0.5),
        "ln_mix_gain": 1.0 + normal(ks[7], (DEPTH, D_MODEL), 0.02),
        "ln_mix_bias": normal(ks[8], (DEPTH, D_MODEL), 0.02),
        "w_peer_query": normal(ks[9], (DEPTH, D_MODEL, PEER_HEADS * PEER_QUERY_DIM), D_MODEL ** -0.5),
        "peer_sub_keys": normal(ks[10], (DEPTH, 2, PEER_N_KEYS, PEER_HALF_DIM), PEER_HALF_DIM ** -0.5),
        "peer_down": normal(ks[11], (DEPTH, PEER_N_EXPERTS, D_MODEL), D_MODEL ** -0.5),
        "peer_up": normal(ks[12], (DEPTH, PEER_N_EXPERTS, D_MODEL), DEEPNORM_BETA),
        "ln_ffn_gain": 1.0 + normal(ks[13], (DEPTH, D_MODEL), 0.02),
        "ln_ffn_bias": normal(ks[14], (DEPTH, D_MODEL), 0.02),
    }


def reference(x, w_in, attn_sinks, ret_gn_gain, w_attn_branch, w_ret_branch, w_out, ln_mix_gain,
              ln_mix_bias, w_peer_query, peer_sub_keys, peer_down, peer_up, ln_ffn_gain, ln_ffn_bias):
    B, S, _ = x.shape
    offsets = np.cumsum(IN_SPLITS)[:-1].tolist()
    for l in range(DEPTH):
        h = x @ w_in[l]
        q_a, k_a, v_a, q_r, k_r, v_r, g_r, gate_a, gate_b = jnp.split(h, offsets, axis=-1)
        attn = sliding_window_attention(
            q_a.reshape(B, S, ATTN_HEADS, HEAD_DIM) * HEAD_DIM ** -0.5,
            k_a.reshape(B, S, ATTN_KV_HEADS, HEAD_DIM),
            v_a.reshape(B, S, ATTN_KV_HEADS, HEAD_DIM),
            attn_sinks[l])
        ret = retention_chunkwise(
            q_r.reshape(B, S, RET_HEADS, RET_HEAD_DIM) * RET_HEAD_DIM ** -0.5,
            k_r.reshape(B, S, RET_HEADS, RET_HEAD_DIM),
            v_r.reshape(B, S, RET_HEADS, RET_HEAD_DIM))
        ret = (head_group_norm(ret, ret_gn_gain[l]) * jax.nn.silu(g_r.astype(jnp.float32))).astype(x.dtype)
        merged = (jax.nn.sigmoid(gate_a) * (attn @ w_attn_branch[l])
                  + jax.nn.sigmoid(gate_b) * (ret @ w_ret_branch[l]))
        x = layer_norm(DEEPNORM_ALPHA * x + merged @ w_out[l], ln_mix_gain[l], ln_mix_bias[l])
        y = peer(x, w_peer_query[l], peer_sub_keys[l], peer_down[l], peer_up[l])
        x = layer_norm(DEEPNORM_ALPHA * x + y, ln_ffn_gain[l], ln_ffn_bias[l])
    return x
```

```python
import functools

import jax
import jax.numpy as jnp
import numpy as np
from jax import lax
from jax.experimental import pallas as pl
from jax.experimental.pallas import tpu as pltpu

D_MODEL = 1024
HEAD_DIM = 64
ATTN_HEADS = 8
ATTN_KV_HEADS = 2
ATTN_GROUP = ATTN_HEADS // ATTN_KV_HEADS
WINDOW = 128
ATTN_WIDTH = ATTN_HEADS * HEAD_DIM
ATTN_KV_WIDTH = ATTN_KV_HEADS * HEAD_DIM
RET_HEADS = 4
RET_HEAD_DIM = 128
RET_WIDTH = RET_HEADS * RET_HEAD_DIM
RET_CHUNK = 128
PEER_HEADS = 8
PEER_N_KEYS = 128
PEER_N_EXPERTS = PEER_N_KEYS * PEER_N_KEYS
PEER_QUERY_DIM = 256
PEER_HALF_DIM = PEER_QUERY_DIM // 2
PEER_TOPK = 16
LN_EPS = 1e-5
GN_EPS = 1e-5
IN_SPLITS = (ATTN_WIDTH, ATTN_KV_WIDTH, ATTN_KV_WIDTH, RET_WIDTH, RET_WIDTH, RET_WIDTH, RET_WIDTH,
             D_MODEL, D_MODEL)
IN_OFFSETS = tuple(int(v) for v in np.concatenate([[0], np.cumsum(IN_SPLITS)]))
IN_WIDTH = IN_OFFSETS[-1]

SUBLANES = 8
LANES = 128

MIX_TOKENS = 512
ROUTE_TOKENS = 512
PEER_TOKENS = 512
PEER_EXPERT_TILE = 1024
VMEM_LIMIT = 52 * 1024 * 1024

BF16 = jnp.bfloat16
F32 = jnp.float32


def _dot(a, b):
    return jnp.dot(a, b, preferred_element_type=F32)


def _dot_nt(a, b):
    return lax.dot_general(a, b, (((1,), (1,)), ((), ())), preferred_element_type=F32)


def _dot_tn(a, b):
    return lax.dot_general(a, b, (((0,), (0,)), ((), ())), preferred_element_type=F32)


def _layer_norm(z, gain, bias):
    mu = jnp.mean(z, axis=-1, keepdims=True)
    zc = z - mu
    var = jnp.mean(zc * zc, axis=-1, keepdims=True)
    return zc * lax.rsqrt(var + LN_EPS) * gain + bias


def _mixer_kernel(alpha, sinks_ref, x_ref, w_in_ref, abias_ref, dmat_ref, qdec_ref, kdec_ref, cdec_ref,
                  gn_ref, wab_ref, wrb_ref, wout_ref, lng_ref, lnb_ref,
                  x1_ref, x1b_ref,
                  kbuf_ref, vbuf_ref, state_ref, attn_ref, ret_ref):
    seq_step = pl.program_id(1)
    n_chunks = MIX_TOKENS // WINDOW

    @pl.when(seq_step == 0)
    def _():
        kbuf_ref[0:WINDOW, :] = jnp.zeros((WINDOW, ATTN_KV_WIDTH), BF16)
        vbuf_ref[0:WINDOW, :] = jnp.zeros((WINDOW, ATTN_KV_WIDTH), BF16)
        state_ref[...] = jnp.zeros(state_ref.shape, F32)

    x = x_ref[...]
    xb = x.astype(BF16)

    def proj(i):
        return _dot(xb, w_in_ref[:, IN_OFFSETS[i]:IN_OFFSETS[i + 1]])

    q_a = (proj(0) * (HEAD_DIM ** -0.5)).astype(BF16)
    kbuf_ref[WINDOW:, :] = proj(1).astype(BF16)
    vbuf_ref[WINDOW:, :] = proj(2).astype(BF16)
    col = lax.broadcasted_iota(jnp.int32, (WINDOW, 2 * WINDOW), 1)
    first_mask = jnp.logical_or(col >= WINDOW, seq_step > 0)
    for c in range(n_chunks):
        rows = slice(c * WINDOW, (c + 1) * WINDOW)
        k_cat = kbuf_ref[c * WINDOW:(c + 2) * WINDOW, :]
        v_cat = vbuf_ref[c * WINDOW:(c + 2) * WINDOW, :]
        for h in range(ATTN_HEADS):
            g = h // ATTN_GROUP
            q_h = q_a[rows, h * HEAD_DIM:(h + 1) * HEAD_DIM]
            k_g = k_cat[:, g * HEAD_DIM:(g + 1) * HEAD_DIM]
            v_g = v_cat[:, g * HEAD_DIM:(g + 1) * HEAD_DIM]
            s = _dot_nt(q_h, k_g) + abias_ref[h]
            if c == 0:
                s = jnp.where(first_mask, s, -jnp.inf)
            sink = sinks_ref[0, h]
            m = jnp.maximum(jnp.max(s, axis=-1, keepdims=True), sink)
            p = jnp.exp(s - m)
            denom = jnp.sum(p, axis=-1, keepdims=True) + jnp.exp(sink - m)
            o = _dot(p.astype(BF16), v_g) / denom
            attn_ref[rows, h * HEAD_DIM:(h + 1) * HEAD_DIM] = o
    kbuf_ref[0:WINDOW, :] = kbuf_ref[MIX_TOKENS:, :]
    vbuf_ref[0:WINDOW, :] = vbuf_ref[MIX_TOKENS:, :]

    q_r = proj(3) * (RET_HEAD_DIM ** -0.5)
    k_r = proj(4)
    v_r = proj(5).astype(BF16)
    g_r = proj(6)
    for c in range(n_chunks):
        rows = slice(c * RET_CHUNK, (c + 1) * RET_CHUNK)
        for h in range(RET_HEADS):
            cols = slice(h * RET_HEAD_DIM, (h + 1) * RET_HEAD_DIM)
            q = q_r[rows, cols].astype(BF16)
            k = k_r[rows, cols]
            v = v_r[rows, cols]
            state = state_ref[h]
            a = _dot_nt(q, k.astype(BF16)) * dmat_ref[h]
            inner = _dot(a.astype(BF16), v)
            cross = _dot(q, state.astype(BF16)) * qdec_ref[h]
            kd = (k * kdec_ref[h]).astype(BF16)
            state_ref[h] = cdec_ref[h] * state + _dot_tn(kd, v)
            y = inner + cross
            mu = jnp.mean(y, axis=-1, keepdims=True)
            yc = y - mu
            var = jnp.mean(yc * yc, axis=-1, keepdims=True)
            gate = g_r[rows, cols]
            ret_ref[rows, cols] = (yc * lax.rsqrt(var + GN_EPS) * gn_ref[:, cols]
                                   * (gate * jax.nn.sigmoid(gate)))

    branch_a = _dot(attn_ref[...].astype(BF16), wab_ref[...])
    branch_b = _dot(ret_ref[...].astype(BF16), wrb_ref[...])
    merged = jax.nn.sigmoid(proj(7)) * branch_a + jax.nn.sigmoid(proj(8)) * branch_b
    z = alpha * x + _dot(merged.astype(BF16), wout_ref[...])
    x1 = _layer_norm(z, lng_ref[...], lnb_ref[...])
    x1_ref[...] = x1
    x1b_ref[...] = x1.astype(BF16)


def _const_spec(shape):
    zeros = (0,) * len(shape)
    return pl.BlockSpec(shape, lambda *_: zeros, pipeline_mode=pl.Buffered(1))


def _mixer(x, w_in_b, sinks, consts, gn_gain, wab_b, wrb_b, wout_b, ln_gain, ln_bias, alpha):
    batch, seq, d = x.shape
    abias, dmat, qdec, kdec, cdec = consts
    tok_spec = pl.BlockSpec((None, MIX_TOKENS, d), lambda b, s: (b, s, 0))
    return pl.pallas_call(
        functools.partial(_mixer_kernel, alpha),
        grid=(batch, seq // MIX_TOKENS),
        in_specs=[
            pl.BlockSpec(memory_space=pltpu.SMEM),
            tok_spec,
            _const_spec(w_in_b.shape),
            _const_spec(abias.shape), _const_spec(dmat.shape), _const_spec(qdec.shape),
            _const_spec(kdec.shape), _const_spec(cdec.shape),
            _const_spec(gn_gain.shape), _const_spec(wab_b.shape), _const_spec(wrb_b.shape),
            _const_spec(wout_b.shape), _const_spec(ln_gain.shape), _const_spec(ln_bias.shape),
        ],
        out_specs=[tok_spec, tok_spec],
        out_shape=[jax.ShapeDtypeStruct(x.shape, F32), jax.ShapeDtypeStruct(x.shape, BF16)],
        scratch_shapes=[
            pltpu.VMEM((MIX_TOKENS + WINDOW, ATTN_KV_WIDTH), BF16),
            pltpu.VMEM((MIX_TOKENS + WINDOW, ATTN_KV_WIDTH), BF16),
            pltpu.VMEM((RET_HEADS, RET_HEAD_DIM, RET_HEAD_DIM), F32),
            pltpu.VMEM((MIX_TOKENS, ATTN_WIDTH), F32),
            pltpu.VMEM((MIX_TOKENS, RET_WIDTH), F32),
        ],
        compiler_params=pltpu.CompilerParams(
            dimension_semantics=("arbitrary", "arbitrary"), vmem_limit_bytes=VMEM_LIMIT),
        name="mixer",
    )(sinks, x, w_in_b, abias, dmat, qdec, kdec, cdec, gn_gain, wab_b, wrb_b, wout_b, ln_gain, ln_bias)


def _mixer_constants():
    dist = (jnp.arange(WINDOW)[:, None] + WINDOW) - jnp.arange(2 * WINDOW)[None, :]
    in_band = (dist >= 0) & (dist < WINDOW)
    slopes = 2.0 ** (-8.0 * jnp.arange(1, ATTN_HEADS + 1, dtype=F32) / ATTN_HEADS)
    abias = jnp.where(in_band[None], -slopes[:, None, None] * dist.astype(F32)[None], -jnp.inf)
    log_gamma = jnp.log1p(-(2.0 ** (-5.0 - jnp.arange(RET_HEADS, dtype=F32))))
    pos = jnp.arange(RET_CHUNK, dtype=F32)
    diff = pos[:, None] - pos[None, :]
    dmat = jnp.where(diff >= 0, jnp.exp(jnp.maximum(diff, 0.0) * log_gamma[:, None, None]), 0.0)
    k_decay = jnp.exp((RET_CHUNK - 1.0 - pos) * log_gamma[:, None])
    q_decay = jnp.exp((pos + 1.0) * log_gamma[:, None])
    chunk_decay = jnp.exp(RET_CHUNK * log_gamma)
    full = (RET_HEADS, RET_CHUNK, RET_HEAD_DIM)
    qdec = jnp.broadcast_to(q_decay[:, :, None], full)
    kdec = jnp.broadcast_to(k_decay[:, :, None], full)
    cdec = jnp.broadcast_to(chunk_decay[:, None, None], full)
    return abias.astype(F32), dmat.astype(F32), qdec, kdec, cdec


def _batcher_pairs(n):
    pairs = []

    def merge(lo, hi, r):
        step = r * 2
        if step < hi - lo:
            merge(lo, hi, step)
            merge(lo + r, hi, step)
            pairs.extend((i, i + r) for i in range(lo + r, hi - r, step))
        else:
            pairs.append((lo, lo + r))

    def sort(lo, hi):
        if hi - lo >= 1:
            mid = lo + (hi - lo) // 2
            sort(lo, mid)
            sort(mid + 1, hi)
            merge(lo, hi, 1)

    sort(0, n - 1)
    return pairs


_SORT16 = _batcher_pairs(PEER_TOPK)


def _exchange(v, i, j):
    hi, lo = jnp.maximum(v[i], v[j]), jnp.minimum(v[i], v[j])
    v[i], v[j] = hi, lo


def _merge_sublanes(v):
    n = len(v)
    for shift in (4, 2, 1):
        other = [pltpu.roll(a, shift, axis=0) for a in v]
        v = [jnp.maximum(v[i], other[n - 1 - i]) for i in range(n)]
        k = n // 2
        while k >= 1:
            for i in range(n):
                if i & k == 0:
                    _exchange(v, i, i + k)
            k //= 2
    return v


def _top16(s):
    v = [s[SUBLANES * m:SUBLANES * (m + 1), :] for m in range(PEER_N_KEYS // SUBLANES)]
    for i, j in _SORT16:
        _exchange(v, i, j)
    return _merge_sublanes(v)


def _sublane_total(a):
    a = a + pltpu.roll(a, 4, axis=0)
    a = a + pltpu.roll(a, 2, axis=0)
    return a + pltpu.roll(a, 1, axis=0)


def _route_kernel(xb_ref, wq_ref, keys_ref, s2_ref, e2_ref, thr_ref, c_ref):
    q = _dot(xb_ref[...], wq_ref[...])
    q1 = q[:, :PEER_HALF_DIM].astype(BF16)
    q2 = q[:, PEER_HALF_DIM:].astype(BF16)
    s1_all = _dot_nt(keys_ref[0], q1)
    s2_all = _dot_nt(keys_ref[1], q2)
    sub = lax.broadcasted_iota(jnp.int32, (SUBLANES, LANES), 0)
    n_slabs = PEER_N_KEYS // SUBLANES
    for g in range(ROUTE_TOKENS // LANES):
        lanes = slice(g * LANES, (g + 1) * LANES)
        s1 = s1_all[:, lanes]
        s2 = s2_all[:, lanes]
        v1 = _top16(s1)
        v2 = _top16(s2)
        b_lo, b_hi = v2[7], v2[15]
        for r in range(6, -1, -1):
            b_lo = jnp.where(sub == r, v2[r], b_lo)
            b_hi = jnp.where(sub == r, v2[8 + r], b_hi)
        cand = [v1[a] + b_lo for a in range(PEER_TOPK)]
        extra = v1[0] + b_hi
        top = list(cand)
        carry = extra
        for i in range(PEER_TOPK):
            top[i], carry = jnp.maximum(top[i], carry), jnp.minimum(top[i], carry)
        tau = _merge_sublanes(top)[PEER_TOPK - 1]
        peak = v1[0] + v2[0]
        zsum = jnp.where(extra >= tau, jnp.exp(extra - peak), 0.0)
        for a in range(PEER_TOPK):
            zsum = zsum + jnp.where(cand[a] >= tau, jnp.exp(cand[a] - peak), 0.0)
        inv_z = 1.0 / _sublane_total(zsum)
        for m in range(n_slabs):
            rows = slice(SUBLANES * m, SUBLANES * (m + 1))
            s1_m = s1[rows, :]
            s2_m = s2[rows, :]
            thr = jnp.full((SUBLANES, LANES), jnp.inf, F32)
            for b in range(PEER_TOPK):
                thr = jnp.where(s1_m + v2[b] >= tau, v2[b], thr)
            thr_ref[rows, lanes] = thr
            c_ref[rows, lanes] = jnp.exp(s1_m - v1[0]) * inv_z
            s2_ref[rows, lanes] = s2_m
            e2_ref[rows, lanes] = jnp.exp(s2_m - v2[0])


def _route(x1b, wq_b, keys_b):
    tokens, d = x1b.shape
    out_spec = pl.BlockSpec((None, PEER_N_KEYS, ROUTE_TOKENS), lambda i, h: (h, 0, i))
    out_shape = jax.ShapeDtypeStruct((PEER_HEADS, PEER_N_KEYS, tokens), F32)
    return pl.pallas_call(
        _route_kernel,
        grid=(tokens // ROUTE_TOKENS, PEER_HEADS),
        in_specs=[
            pl.BlockSpec((ROUTE_TOKENS, d), lambda i, h: (i, 0)),
            pl.BlockSpec((d, PEER_QUERY_DIM), lambda i, h: (0, h)),
            pl.BlockSpec(keys_b.shape, lambda i, h: (0, 0, 0)),
        ],
        out_specs=[out_spec] * 4,
        out_shape=[out_shape] * 4,
        compiler_params=pltpu.CompilerParams(
            dimension_semantics=("arbitrary", "arbitrary"), vmem_limit_bytes=VMEM_LIMIT),
        name="route",
    )(x1b, wq_b, keys_b)


def _peer_kernel(alpha, xb_ref, x_ref, s2_ref, e2_ref, thr_ref, c_ref, down_ref, up_t_ref, lng_ref, lnb_ref,
                 out_ref, acc_ref, act_ref, h_ref):
    k = pl.program_id(1)

    @pl.when(k == 0)
    def _():
        acc_ref[...] = jnp.zeros(acc_ref.shape, F32)

    act_ref[...] = _dot_nt(down_ref[...], xb_ref[...])
    for jj in range(PEER_EXPERT_TILE // PEER_N_KEYS):
        rows = slice(jj * PEER_N_KEYS, (jj + 1) * PEER_N_KEYS)
        a = act_ref[rows, :]
        gelu = 0.5 * a * (1.0 + lax.erf(a * (2.0 ** -0.5)))
        gate = jnp.zeros_like(a)
        for h in range(PEER_HEADS):
            thr = thr_ref[h, jj:jj + 1, :]
            scale = c_ref[h, jj:jj + 1, :]
            gate = gate + jnp.where(s2_ref[h] >= thr, e2_ref[h], 0.0) * scale
        h_ref[rows, :] = (gate * gelu).astype(BF16)
    acc_ref[...] += _dot(up_t_ref[...], h_ref[...])

    @pl.when(k == pl.num_programs(1) - 1)
    def _():
        z = alpha * x_ref[...] + acc_ref[...].T
        out_ref[...] = _layer_norm(z, lng_ref[...], lnb_ref[...])


def _peer(x1b, x1, route, down_b, up_t_b, ln_gain, ln_bias, alpha):
    tokens, d = x1.shape
    s2, e2, thr, scale = route
    j_per_step = PEER_EXPERT_TILE // PEER_N_KEYS
    tok_spec = pl.BlockSpec((PEER_TOKENS, d), lambda i, k: (i, 0))
    full_spec = pl.BlockSpec((PEER_HEADS, PEER_N_KEYS, PEER_TOKENS), lambda i, k: (0, 0, i))
    row_spec = pl.BlockSpec((PEER_HEADS, j_per_step, PEER_TOKENS), lambda i, k: (0, k, i))
    vec_spec = pl.BlockSpec((1, d), lambda i, k: (0, 0))
    return pl.pallas_call(
        functools.partial(_peer_kernel, alpha),
        grid=(tokens // PEER_TOKENS, PEER_N_EXPERTS // PEER_EXPERT_TILE),
        in_specs=[
            tok_spec, tok_spec, full_spec, full_spec, row_spec, row_spec,
            pl.BlockSpec((PEER_EXPERT_TILE, d), lambda i, k: (k, 0)),
            pl.BlockSpec((d, PEER_EXPERT_TILE), lambda i, k: (0, k)),
            vec_spec, vec_spec,
        ],
        out_specs=tok_spec,
        out_shape=jax.ShapeDtypeStruct((tokens, d), F32),
        scratch_shapes=[
            pltpu.VMEM((d, PEER_TOKENS), F32),
            pltpu.VMEM((PEER_EXPERT_TILE, PEER_TOKENS), F32),
            pltpu.VMEM((PEER_EXPERT_TILE, PEER_TOKENS), BF16),
        ],
        compiler_params=pltpu.CompilerParams(
            dimension_semantics=("arbitrary", "arbitrary"), vmem_limit_bytes=VMEM_LIMIT),
        name="peer",
    )(x1b, x1, s2, e2, thr, scale, down_b, up_t_b, ln_gain, ln_bias)


def kernel(x, w_in, attn_sinks, ret_gn_gain, w_attn_branch, w_ret_branch, w_out, ln_mix_gain, ln_mix_bias,
           w_peer_query, peer_sub_keys, peer_down, peer_up, ln_ffn_gain, ln_ffn_bias):
    batch, seq, d = x.shape
    depth = w_in.shape[0]
    alpha = float((2 * depth) ** 0.25)
    consts = _mixer_constants()
    for l in range(depth):
        x1, x1b = _mixer(
            x, w_in[l].astype(BF16), attn_sinks[l][None, :], consts, ret_gn_gain[l][None, :],
            w_attn_branch[l].astype(BF16), w_ret_branch[l].astype(BF16), w_out[l].astype(BF16),
            ln_mix_gain[l][None, :], ln_mix_bias[l][None, :], alpha)
        x1 = x1.reshape(batch * seq, d)
        x1b = x1b.reshape(batch * seq, d)
        route = _route(x1b, w_peer_query[l].astype(BF16), peer_sub_keys[l].astype(BF16))
        out = _peer(x1b, x1, route, peer_down[l].astype(BF16), peer_up[l].T.astype(BF16),
                    ln_ffn_gain[l][None, :], ln_ffn_bias[l][None, :], alpha)
        x = out.reshape(batch, seq, d)
    return x
```

```python
import functools

import jax
import jax.numpy as jnp
import numpy as np
from jax import lax
from jax.experimental import pallas as pl
from jax.experimental.pallas import tpu as pltpu

D_MODEL = 1024
HEAD_DIM = 64
ATTN_HEADS = 8
ATTN_KV_HEADS = 2
ATTN_GROUP = ATTN_HEADS // ATTN_KV_HEADS
WINDOW = 128
ATTN_WIDTH = ATTN_HEADS * HEAD_DIM
ATTN_KV_WIDTH = ATTN_KV_HEADS * HEAD_DIM
RET_HEADS = 4
RET_HEAD_DIM = 128
RET_WIDTH = RET_HEADS * RET_HEAD_DIM
RET_CHUNK = 128
PEER_HEADS = 8
PEER_N_KEYS = 128
PEER_N_EXPERTS = PEER_N_KEYS * PEER_N_KEYS
PEER_QUERY_DIM = 256
PEER_HALF_DIM = PEER_QUERY_DIM // 2
PEER_TOPK = 16
LN_EPS = 1e-5
GN_EPS = 1e-5
IN_SPLITS = (ATTN_WIDTH, ATTN_KV_WIDTH, ATTN_KV_WIDTH, RET_WIDTH, RET_WIDTH, RET_WIDTH, RET_WIDTH,
             D_MODEL, D_MODEL)
IN_OFFSETS = tuple(int(v) for v in np.concatenate([[0], np.cumsum(IN_SPLITS)]))
IN_WIDTH = IN_OFFSETS[-1]

SUBLANES = 8
LANES = 128

MIX_TOKENS = 512
ROUTE_TOKENS = 512
PEER_TOKENS = 512
PEER_EXPERT_TILE = 1024
PEER_TILES_PER_TOKEN_TILE = PEER_N_EXPERTS // PEER_EXPERT_TILE
VMEM_LIMIT = 52 * 1024 * 1024

BF16 = jnp.bfloat16
F32 = jnp.float32


def _dot(a, b):
    return jnp.dot(a, b, preferred_element_type=F32)


def _dot_nt(a, b):
    return lax.dot_general(a, b, (((1,), (1,)), ((), ())), preferred_element_type=F32)


def _dot_tn(a, b):
    return lax.dot_general(a, b, (((0,), (0,)), ((), ())), preferred_element_type=F32)


def _layer_norm(z, gain, bias):
    mu = jnp.mean(z, axis=-1, keepdims=True)
    zc = z - mu
    var = jnp.mean(zc * zc, axis=-1, keepdims=True)
    return zc * lax.rsqrt(var + LN_EPS) * gain + bias


def _mixer_kernel(alpha, sinks_ref, x_ref, w_in_ref, abias_ref, dmat_ref, qdec_ref, kdec_ref, cdec_ref,
                  gn_ref, wab_ref, wrb_ref, wout_ref, lng_ref, lnb_ref,
                  x1_ref, x1b_ref,
                  kbuf_ref, vbuf_ref, state_ref, attn_ref, ret_ref):
    seq_step = pl.program_id(1)
    n_chunks = MIX_TOKENS // WINDOW

    @pl.when(seq_step == 0)
    def _():
        kbuf_ref[0:WINDOW, :] = jnp.zeros((WINDOW, ATTN_KV_WIDTH), BF16)
        vbuf_ref[0:WINDOW, :] = jnp.zeros((WINDOW, ATTN_KV_WIDTH), BF16)
        state_ref[...] = jnp.zeros(state_ref.shape, F32)

    x = x_ref[...]
    xb = x.astype(BF16)

    def proj(i):
        return _dot(xb, w_in_ref[:, IN_OFFSETS[i]:IN_OFFSETS[i + 1]])

    q_a = (proj(0) * (HEAD_DIM ** -0.5)).astype(BF16)
    kbuf_ref[WINDOW:, :] = proj(1).astype(BF16)
    vbuf_ref[WINDOW:, :] = proj(2).astype(BF16)
    col = lax.broadcasted_iota(jnp.int32, (WINDOW, 2 * WINDOW), 1)
    first_mask = jnp.logical_or(col >= WINDOW, seq_step > 0)
    for c in range(n_chunks):
        rows = slice(c * WINDOW, (c + 1) * WINDOW)
        k_cat = kbuf_ref[c * WINDOW:(c + 2) * WINDOW, :]
        v_cat = vbuf_ref[c * WINDOW:(c + 2) * WINDOW, :]
        for h in range(ATTN_HEADS):
            g = h // ATTN_GROUP
            q_h = q_a[rows, h * HEAD_DIM:(h + 1) * HEAD_DIM]
            k_g = k_cat[:, g * HEAD_DIM:(g + 1) * HEAD_DIM]
            v_g = v_cat[:, g * HEAD_DIM:(g + 1) * HEAD_DIM]
            s = _dot_nt(q_h, k_g) + abias_ref[h]
            if c == 0:
                s = jnp.where(first_mask, s, -jnp.inf)
            sink = sinks_ref[0, h]
            m = jnp.maximum(jnp.max(s, axis=-1, keepdims=True), sink)
            p = jnp.exp(s - m)
            denom = jnp.sum(p, axis=-1, keepdims=True) + jnp.exp(sink - m)
            o = _dot(p.astype(BF16), v_g) / denom
            attn_ref[rows, h * HEAD_DIM:(h + 1) * HEAD_DIM] = o
    kbuf_ref[0:WINDOW, :] = kbuf_ref[MIX_TOKENS:, :]
    vbuf_ref[0:WINDOW, :] = vbuf_ref[MIX_TOKENS:, :]

    q_r = proj(3) * (RET_HEAD_DIM ** -0.5)
    k_r = proj(4)
    v_r = proj(5).astype(BF16)
    g_r = proj(6)
    for c in range(n_chunks):
        rows = slice(c * RET_CHUNK, (c + 1) * RET_CHUNK)
        for h in range(RET_HEADS):
            cols = slice(h * RET_HEAD_DIM, (h + 1) * RET_HEAD_DIM)
            q = q_r[rows, cols].astype(BF16)
            k = k_r[rows, cols]
            v = v_r[rows, cols]
            state = state_ref[h]
            a = _dot_nt(q, k.astype(BF16)) * dmat_ref[h]
            inner = _dot(a.astype(BF16), v)
            cross = _dot(q, state.astype(BF16)) * qdec_ref[h]
            kd = (k * kdec_ref[h]).astype(BF16)
            state_ref[h] = cdec_ref[h] * state + _dot_tn(kd, v)
            y = inner + cross
            mu = jnp.mean(y, axis=-1, keepdims=True)
            yc = y - mu
            var = jnp.mean(yc * yc, axis=-1, keepdims=True)
            gate = g_r[rows, cols]
            ret_ref[rows, cols] = (yc * lax.rsqrt(var + GN_EPS) * gn_ref[:, cols]
                                   * (gate * jax.nn.sigmoid(gate)))

    branch_a = _dot(attn_ref[...].astype(BF16), wab_ref[...])
    branch_b = _dot(ret_ref[...].astype(BF16), wrb_ref[...])
    merged = jax.nn.sigmoid(proj(7)) * branch_a + jax.nn.sigmoid(proj(8)) * branch_b
    z = alpha * x + _dot(merged.astype(BF16), wout_ref[...])
    x1 = _layer_norm(z, lng_ref[...], lnb_ref[...])
    x1_ref[...] = x1
    x1b_ref[...] = x1.astype(BF16)


def _const_spec(shape):
    zeros = (0,) * len(shape)
    return pl.BlockSpec(shape, lambda *_: zeros, pipeline_mode=pl.Buffered(1))


def _mixer(x, w_in_b, sinks, consts, gn_gain, wab_b, wrb_b, wout_b, ln_gain, ln_bias, alpha):
    batch, seq, d = x.shape
    abias, dmat, qdec, kdec, cdec = consts
    tok_spec = pl.BlockSpec((None, MIX_TOKENS, d), lambda b, s: (b, s, 0))
    return pl.pallas_call(
        functools.partial(_mixer_kernel, alpha),
        grid=(batch, seq // MIX_TOKENS),
        in_specs=[
            pl.BlockSpec(memory_space=pltpu.SMEM),
            tok_spec,
            _const_spec(w_in_b.shape),
            _const_spec(abias.shape), _const_spec(dmat.shape), _const_spec(qdec.shape),
            _const_spec(kdec.shape), _const_spec(cdec.shape),
            _const_spec(gn_gain.shape), _const_spec(wab_b.shape), _const_spec(wrb_b.shape),
            _const_spec(wout_b.shape), _const_spec(ln_gain.shape), _const_spec(ln_bias.shape),
        ],
        out_specs=[tok_spec, tok_spec],
        out_shape=[jax.ShapeDtypeStruct(x.shape, F32), jax.ShapeDtypeStruct(x.shape, BF16)],
        scratch_shapes=[
            pltpu.VMEM((MIX_TOKENS + WINDOW, ATTN_KV_WIDTH), BF16),
            pltpu.VMEM((MIX_TOKENS + WINDOW, ATTN_KV_WIDTH), BF16),
            pltpu.VMEM((RET_HEADS, RET_HEAD_DIM, RET_HEAD_DIM), F32),
            pltpu.VMEM((MIX_TOKENS, ATTN_WIDTH), F32),
            pltpu.VMEM((MIX_TOKENS, RET_WIDTH), F32),
        ],
        compiler_params=pltpu.CompilerParams(
            dimension_semantics=("arbitrary", "arbitrary"), vmem_limit_bytes=VMEM_LIMIT),
        name="mixer",
    )(sinks, x, w_in_b, abias, dmat, qdec, kdec, cdec, gn_gain, wab_b, wrb_b, wout_b, ln_gain, ln_bias)


def _mixer_constants():
    dist = (jnp.arange(WINDOW)[:, None] + WINDOW) - jnp.arange(2 * WINDOW)[None, :]
    in_band = (dist >= 0) & (dist < WINDOW)
    slopes = 2.0 ** (-8.0 * jnp.arange(1, ATTN_HEADS + 1, dtype=F32) / ATTN_HEADS)
    abias = jnp.where(in_band[None], -slopes[:, None, None] * dist.astype(F32)[None], -jnp.inf)
    log_gamma = jnp.log1p(-(2.0 ** (-5.0 - jnp.arange(RET_HEADS, dtype=F32))))
    pos = jnp.arange(RET_CHUNK, dtype=F32)
    diff = pos[:, None] - pos[None, :]
    dmat = jnp.where(diff >= 0, jnp.exp(jnp.maximum(diff, 0.0) * log_gamma[:, None, None]), 0.0)
    k_decay = jnp.exp((RET_CHUNK - 1.0 - pos) * log_gamma[:, None])
    q_decay = jnp.exp((pos + 1.0) * log_gamma[:, None])
    chunk_decay = jnp.exp(RET_CHUNK * log_gamma)
    full = (RET_HEADS, RET_CHUNK, RET_HEAD_DIM)
    qdec = jnp.broadcast_to(q_decay[:, :, None], full)
    kdec = jnp.broadcast_to(k_decay[:, :, None], full)
    cdec = jnp.broadcast_to(chunk_decay[:, None, None], full)
    return abias.astype(F32), dmat.astype(F32), qdec, kdec, cdec


def _batcher_pairs(n):
    pairs = []

    def merge(lo, hi, r):
        step = r * 2
        if step < hi - lo:
            merge(lo, hi, step)
            merge(lo + r, hi, step)
            pairs.extend((i, i + r) for i in range(lo + r, hi - r, step))
        else:
            pairs.append((lo, lo + r))

    def sort(lo, hi):
        if hi - lo >= 1:
            mid = lo + (hi - lo) // 2
            sort(lo, mid)
            sort(mid + 1, hi)
            merge(lo, hi, 1)

    sort(0, n - 1)
    return pairs


_SORT16 = _batcher_pairs(PEER_TOPK)


def _exchange(v, i, j):
    hi, lo = jnp.maximum(v[i], v[j]), jnp.minimum(v[i], v[j])
    v[i], v[j] = hi, lo


def _merge_sublanes(v):
    n = len(v)
    for shift in (4, 2, 1):
        other = [pltpu.roll(a, shift, axis=0) for a in v]
        v = [jnp.maximum(v[i], other[n - 1 - i]) for i in range(n)]
        k = n // 2
        while k >= 1:
            for i in range(n):
                if i & k == 0:
                    _exchange(v, i, i + k)
            k //= 2
    return v


def _top16(s):
    v = [s[SUBLANES * m:SUBLANES * (m + 1), :] for m in range(PEER_N_KEYS // SUBLANES)]
    for i, j in _SORT16:
        _exchange(v, i, j)
    return _merge_sublanes(v)


def _sublane_total(a):
    a = a + pltpu.roll(a, 4, axis=0)
    a = a + pltpu.roll(a, 2, axis=0)
    return a + pltpu.roll(a, 1, axis=0)


def _route_kernel(xb_ref, wq_ref, keys_ref, rank_ref, e2_ref, count_ref, scale_ref):
    q = _dot(xb_ref[...], wq_ref[...])
    q1 = q[:, :PEER_HALF_DIM].astype(BF16)
    q2 = q[:, PEER_HALF_DIM:].astype(BF16)
    s1_all = _dot_nt(keys_ref[0], q1)
    s2_all = _dot_nt(keys_ref[1], q2)
    sub = lax.broadcasted_iota(jnp.int32, (SUBLANES, LANES), 0)
    n_slabs = PEER_N_KEYS // SUBLANES
    for g in range(ROUTE_TOKENS // LANES):
        lanes = slice(g * LANES, (g + 1) * LANES)
        s1 = s1_all[:, lanes]
        s2 = s2_all[:, lanes]
        v1 = _top16(s1)
        v2 = _top16(s2)
        b_lo, b_hi = v2[7], v2[15]
        for r in range(6, -1, -1):
            b_lo = jnp.where(sub == r, v2[r], b_lo)
            b_hi = jnp.where(sub == r, v2[8 + r], b_hi)
        cand = [v1[a] + b_lo for a in range(PEER_TOPK)]
        extra = v1[0] + b_hi
        top = list(cand)
        carry = extra
        for i in range(PEER_TOPK):
            top[i], carry = jnp.maximum(top[i], carry), jnp.minimum(top[i], carry)
        tau = _merge_sublanes(top)[PEER_TOPK - 1]
        peak = v1[0] + v2[0]
        zsum = jnp.where(extra >= tau, jnp.exp(extra - peak), 0.0)
        for a in range(PEER_TOPK):
            zsum = zsum + jnp.where(cand[a] >= tau, jnp.exp(cand[a] - peak), 0.0)
        inv_z = 1.0 / _sublane_total(zsum)
        ranks, e2s = [], []
        for m in range(n_slabs):
            rows = slice(SUBLANES * m, SUBLANES * (m + 1))
            s1_m = s1[rows, :]
            s2_m = s2[rows, :]
            count = jnp.zeros((SUBLANES, LANES), F32)
            rank = jnp.zeros((SUBLANES, LANES), F32)
            for b in range(PEER_TOPK):
                count = jnp.where(s1_m + v2[b] >= tau, float(b + 1), count)
                rank = jnp.where(v2[b] > s2_m, float(b + 1), rank)
            count_ref[g, rows, :] = count
            scale_ref[g, rows, :] = jnp.exp(s1_m - v1[0]) * inv_z
            ranks.append(rank)
            e2s.append(jnp.exp(s2_m - v2[0]))
        rank_ref[g] = jnp.concatenate(ranks, axis=0).astype(BF16)
        e2_ref[g] = jnp.concatenate(e2s, axis=0).astype(BF16)


def _route(x1b, wq_b, keys_b):
    tokens, d = x1b.shape
    groups = ROUTE_TOKENS // LANES
    out_spec = pl.BlockSpec((None, groups, PEER_N_KEYS, LANES), lambda i, h: (h, i, 0, 0))
    shape = (PEER_HEADS, tokens // LANES, PEER_N_KEYS, LANES)
    return pl.pallas_call(
        _route_kernel,
        grid=(tokens // ROUTE_TOKENS, PEER_HEADS),
        in_specs=[
            pl.BlockSpec((ROUTE_TOKENS, d), lambda i, h: (i, 0)),
            pl.BlockSpec((d, PEER_QUERY_DIM), lambda i, h: (0, h)),
            pl.BlockSpec(keys_b.shape, lambda i, h: (0, 0, 0)),
        ],
        out_specs=[out_spec] * 4,
        out_shape=[jax.ShapeDtypeStruct(shape, BF16), jax.ShapeDtypeStruct(shape, BF16),
                   jax.ShapeDtypeStruct(shape, F32), jax.ShapeDtypeStruct(shape, F32)],
        compiler_params=pltpu.CompilerParams(
            dimension_semantics=("arbitrary", "arbitrary"), vmem_limit_bytes=VMEM_LIMIT),
        name="route",
    )(x1b, wq_b, keys_b)


def _peer_kernel(alpha, n_steps, xb_ref, x_ref, rank_ref, e2_ref, count_ref, scale_ref, down_ref, up_t_ref,
                 lng_ref, lnb_ref, out_ref, acc_ref, act_even, act_odd, h_even, h_odd, rank_buf, e2_buf):
    s = pl.program_id(0)
    tile2 = jnp.clip(s - 2, 0, n_steps - 1) % PEER_TILES_PER_TOKEN_TILE
    restart = tile2 == 0
    slab = 2 * SUBLANES

    @pl.when(jnp.clip(s - 1, 0, n_steps - 1) % PEER_TILES_PER_TOKEN_TILE == 0)
    def _():
        rank_buf[...] = rank_ref[...]
        e2_buf[...] = e2_ref[...]

    @pl.when(s == 0)
    def _():
        act_odd[...] = jnp.zeros(act_odd.shape, F32)
        h_even[...] = jnp.zeros(h_even.shape, BF16)
        acc_ref[...] = jnp.zeros(acc_ref.shape, F32)

    def stages(act_new, act_gate, h_gate, h_prev):
        act_new[...] = _dot_nt(down_ref[...], xb_ref[...])
        for j_row in range(PEER_EXPERT_TILE // PEER_N_KEYS):
            for g in range(PEER_TOKENS // LANES):
                lanes = slice(g * LANES, (g + 1) * LANES)
                count = [jnp.broadcast_to(count_ref[h, g, j_row:j_row + 1, :], (slab, LANES)).astype(BF16)
                         for h in range(PEER_HEADS)]
                scale = [jnp.broadcast_to(scale_ref[h, g, j_row:j_row + 1, :], (slab, LANES)).astype(BF16)
                         for h in range(PEER_HEADS)]
                for m in range(PEER_N_KEYS // slab):
                    keys = slice(m * slab, (m + 1) * slab)
                    e_rows = slice(j_row * PEER_N_KEYS + m * slab, j_row * PEER_N_KEYS + (m + 1) * slab)
                    a = act_gate[e_rows, lanes]
                    gelu = (0.5 * a * (1.0 + lax.erf(a * (2.0 ** -0.5)))).astype(BF16)
                    gate = None
                    for h in range(PEER_HEADS):
                        chosen = rank_buf[h, g, keys, :] < count[h]
                        term = jnp.where(chosen, e2_buf[h, g, keys, :], jnp.zeros((), BF16)) * scale[h]
                        gate = term if gate is None else gate + term
                    h_gate[e_rows, lanes] = gate * gelu
        prev = jnp.where(restart, 0.0, acc_ref[...])
        acc_ref[...] = prev + _dot(up_t_ref[...], h_prev[...])

    @pl.when(s % 2 == 0)
    def _():
        stages(act_even, act_odd, h_odd, h_even)

    @pl.when(s % 2 == 1)
    def _():
        stages(act_odd, act_even, h_even, h_odd)

    @pl.when(jnp.logical_and(s >= 2, tile2 == PEER_TILES_PER_TOKEN_TILE - 1))
    def _():
        z = alpha * x_ref[...] + acc_ref[...].T
        out_ref[...] = _layer_norm(z, lng_ref[...], lnb_ref[...])


def _peer(x1b, x1, route, down_b, up_t_b, ln_gain, ln_bias, alpha):
    tokens, d = x1.shape
    rank, e2, count, scale = route
    n_steps = (tokens // PEER_TOKENS) * PEER_TILES_PER_TOKEN_TILE
    last = n_steps - 1
    groups = PEER_TOKENS // LANES
    j_per_step = PEER_EXPERT_TILE // PEER_N_KEYS
    per = PEER_TILES_PER_TOKEN_TILE

    def lag(s, k):
        return jnp.clip(s - k, 0, last)

    full_spec = pl.BlockSpec((PEER_HEADS, groups, PEER_N_KEYS, LANES), lambda s: (0, lag(s, 1) // per, 0, 0))
    row_spec = pl.BlockSpec((PEER_HEADS, groups, j_per_step, LANES),
                            lambda s: (0, lag(s, 1) // per, lag(s, 1) % per, 0))
    vec_spec = pl.BlockSpec((1, d), lambda s: (0, 0))
    return pl.pallas_call(
        functools.partial(_peer_kernel, alpha, n_steps),
        grid=(n_steps + 2,),
        in_specs=[
            pl.BlockSpec((PEER_TOKENS, d), lambda s: (lag(s, 0) // per, 0)),
            pl.BlockSpec((PEER_TOKENS, d), lambda s: (lag(s, 2) // per, 0)),
            full_spec, full_spec, row_spec, row_spec,
            pl.BlockSpec((PEER_EXPERT_TILE, d), lambda s: (lag(s, 0) % per, 0)),
            pl.BlockSpec((d, PEER_EXPERT_TILE), lambda s: (0, lag(s, 2) % per)),
            vec_spec, vec_spec,
        ],
        out_specs=pl.BlockSpec((PEER_TOKENS, d), lambda s: (lag(s, 2) // per, 0)),
        out_shape=jax.ShapeDtypeStruct((tokens, d), F32),
        scratch_shapes=[
            pltpu.VMEM((d, PEER_TOKENS), F32),
            pltpu.VMEM((PEER_EXPERT_TILE, PEER_TOKENS), F32),
            pltpu.VMEM((PEER_EXPERT_TILE, PEER_TOKENS), F32),
            pltpu.VMEM((PEER_EXPERT_TILE, PEER_TOKENS), BF16),
            pltpu.VMEM((PEER_EXPERT_TILE, PEER_TOKENS), BF16),
            pltpu.VMEM((PEER_HEADS, groups, PEER_N_KEYS, LANES), BF16),
            pltpu.VMEM((PEER_HEADS, groups, PEER_N_KEYS, LANES), BF16),
        ],
        compiler_params=pltpu.CompilerParams(
            dimension_semantics=("arbitrary",), vmem_limit_bytes=VMEM_LIMIT),
        name="peer",
    )(x1b, x1, rank, e2, count, scale, down_b, up_t_b, ln_gain, ln_bias)


def kernel(x, w_in, attn_sinks, ret_gn_gain, w_attn_branch, w_ret_branch, w_out, ln_mix_gain, ln_mix_bias,
           w_peer_query, peer_sub_keys, peer_down, peer_up, ln_ffn_gain, ln_ffn_bias):
    batch, seq, d = x.shape
    depth = w_in.shape[0]
    alpha = float((2 * depth) ** 0.25)
    consts = _mixer_constants()
    for l in range(depth):
        x1, x1b = _mixer(
            x, w_in[l].astype(BF16), attn_sinks[l][None, :], consts, ret_gn_gain[l][None, :],
            w_attn_branch[l].astype(BF16), w_ret_branch[l].astype(BF16), w_out[l].astype(BF16),
            ln_mix_gain[l][None, :], ln_mix_bias[l][None, :], alpha)
        x1 = x1.reshape(batch * seq, d)
        x1b = x1b.reshape(batch * seq, d)
        route = _route(x1b, w_peer_query[l].astype(BF16), peer_sub_keys[l].astype(BF16))
        out = _peer(x1b, x1, route, peer_down[l].astype(BF16), peer_up[l].T.astype(BF16),
                    ln_ffn_gain[l][None, :], ln_ffn_bias[l][None, :], alpha)
        x = out.reshape(batch, seq, d)
    return x
```
